```python
import math
import jax, jax.numpy as jnp
from jax import lax
import numpy as np

D_MODEL = 1024
BATCH = 8
SEQ = 4096
DEPTH = 2

D_MIX = D_MODEL
D_A = D_MIX // 2
D_B = D_MIX - D_A
GROUP_DIM = 64
K_A = 3
K_B = 31
N_MEM = 256
N_XHEADS = 4
XHEAD_DIM = D_MODEL // N_XHEADS
D_FF = 4 * D_MODEL
D_IN = 3 * D_A + 2 * D_B
EPS = 1e-6

kernel_name = "hybrid_shortconv_conformer_memxattn"


def rms_norm(x, g):
    x32 = x.astype(jnp.float32)
    y = x32 * lax.rsqrt(jnp.mean(x32 * x32, axis=-1, keepdims=True) + EPS)
    return (y * g.astype(jnp.float32)).astype(x.dtype)


def layer_norm(x, g, b):
    x32 = x.astype(jnp.float32)
    mu = jnp.mean(x32, axis=-1, keepdims=True)
    var = jnp.mean(jnp.square(x32 - mu), axis=-1, keepdims=True)
    y = (x32 - mu) * lax.rsqrt(var + EPS)
    return (y * g.astype(jnp.float32) + b.astype(jnp.float32)).astype(x.dtype)


def causal_dwconv(x, w):
    k, c = w.shape
    return lax.conv_general_dilated(
        x, w.reshape(k, 1, c).astype(x.dtype),
        window_strides=(1,), padding=[(k - 1, 0)],
        dimension_numbers=("NWC", "WIO", "NWC"),
        feature_group_count=c)


def mixer_block(u, w_in, conv_a_w, conv_b_w, conv_b_bias, ln_b_g, ln_b_b, w_out):
    z = jnp.einsum("bsd,de->bse", u, w_in)
    b_a, c_a, h_a, v_b, g_b = jnp.split(
        z, [D_A, 2 * D_A, 3 * D_A, 3 * D_A + D_B], axis=-1)
    y_a = b_a * causal_dwconv(c_a * h_a, conv_a_w)
    glu = v_b * jax.nn.sigmoid(g_b)
    cb = causal_dwconv(glu, conv_b_w) + conv_b_bias.astype(glu.dtype)
    y_b = jax.nn.silu(layer_norm(cb, ln_b_g, ln_b_b))
    y = jnp.concatenate([y_a, y_b], axis=-1)
    return jnp.einsum("bse,ed->bsd", y, w_out)


def memory_cross_attention(q_in, m, w_q, w_kv, w_xo):
    b, s, _ = q_in.shape
    q = jnp.einsum("bsd,de->bse", q_in, w_q).reshape(b, s, N_XHEADS, XHEAD_DIM)
    kv = jnp.einsum("bmd,de->bme", m, w_kv)
    k, v = jnp.split(kv, 2, axis=-1)
    k = k.reshape(b, N_MEM, N_XHEADS, XHEAD_DIM)
    v = v.reshape(b, N_MEM, N_XHEADS, XHEAD_DIM)
    scores = jnp.einsum("bshd,bmhd->bhsm", q, k).astype(jnp.float32) * (1.0 / math.sqrt(XHEAD_DIM))
    p = jax.nn.softmax(scores, axis=-1).astype(v.dtype)
    o = jnp.einsum("bhsm,bmhd->bshd", p, v).reshape(b, s, D_MODEL)
    return jnp.einsum("bse,ed->bsd", o, w_xo)


def sqrelu_mlp(u, w_up, w_down):
    h = jnp.square(jax.nn.relu(jnp.einsum("bsd,df->bsf", u, w_up)))
    return jnp.einsum("bsf,fd->bsd", h, w_down)


def setup_inputs(seed: int = 0) -> dict:
    key = jax.random.key(seed)
    ks = jax.random.split(key, 20)
    f32 = jnp.float32

    def nrm(k, shape, scale):
        return jax.random.normal(k, shape, f32) * scale

    def gain(k, shape):
        return 1.0 + 0.02 * jax.random.normal(k, shape, f32)

    res_scale = (2.0 * DEPTH) ** -0.5
    return {
        "x": nrm(ks[0], (BATCH, SEQ, D_MODEL), 1.0),
        "mem": nrm(ks[1], (BATCH, N_MEM, D_MODEL), 1.0),
        "norm_mix_g": gain(ks[2], (DEPTH, D_MODEL)),
        "w_in": nrm(ks[3], (DEPTH, D_MODEL, D_IN), D_MODEL ** -0.5),
        "conv_a_w": nrm(ks[4], (DEPTH, K_A, D_A), K_A ** -0.5),
        "conv_b_w": nrm(ks[5], (DEPTH, K_B, D_B), K_B ** -0.5),
        "conv_b_bias": nrm(ks[6], (DEPTH, D_B), 0.02),
        "ln_b_g": gain(ks[7], (DEPTH, D_B)),
        "ln_b_b": nrm(ks[8], (DEPTH, D_B), 0.02),
        "w_out": nrm(ks[9], (DEPTH, D_MIX, D_MODEL), D_MIX ** -0.5 * res_scale),
        "norm_x_g": gain(ks[10], (DEPTH, D_MODEL)),
        "norm_mem_g": gain(ks[11], (DEPTH, D_MODEL)),
        "w_q": nrm(ks[12], (DEPTH, D_MODEL, D_MODEL), D_MODEL ** -0.5),
        "w_kv": nrm(ks[13], (DEPTH, D_MODEL, 2 * D_MODEL), D_MODEL ** -0.5),
        "w_xo": nrm(ks[14], (DEPTH, D_MODEL, D_MODEL), D_MODEL ** -0.5 * res_scale),
        "norm_ffn_g": gain(ks[15], (DEPTH, D_MODEL)),
        "w_up": nrm(ks[16], (DEPTH, D_MODEL, D_FF), D_MODEL ** -0.5),
        "w_down": nrm(ks[17], (DEPTH, D_FF, D_MODEL), D_FF ** -0.5 * res_scale),
        "final_g": gain(ks[18], (D_MODEL,)),
    }


def reference(x, mem, norm_mix_g, w_in, conv_a_w, conv_b_w, conv_b_bias, ln_b_g, ln_b_b, w_out,
              norm_x_g, norm_mem_g, w_q, w_kv, w_xo, norm_ffn_g, w_up, w_down, final_g):
    h = x
    for l in range(DEPTH):
        u = rms_norm(h, norm_mix_g[l])
        h = h + mixer_block(u, w_in[l], conv_a_w[l], conv_b_w[l], conv_b_bias[l],
                            ln_b_g[l], ln_b_b[l], w_out[l])
        q_in = rms_norm(h, norm_x_g[l])
        m = rms_norm(mem, norm_mem_g[l])
        h = h + memory_cross_attention(q_in, m, w_q[l], w_kv[l], w_xo[l])
        h = h + sqrelu_mlp(rms_norm(h, norm_ffn_g[l]), w_up[l], w_down[l])
    return rms_norm(h, final_g)
```

```python
import functools
import math

import jax
import jax.numpy as jnp
from jax.experimental import pallas as pl
from jax.experimental.pallas import tpu as pltpu

D_MODEL = 1024
D_A = 512
D_B = 512
K_A = 3
K_B = 31
N_MEM = 256
N_XHEADS = 4
XHEAD_DIM = D_MODEL // N_XHEADS
D_FF = 4 * D_MODEL
EPS = 1e-6

F32 = jnp.float32
BF16 = jnp.bfloat16

TS_MIX = 512
TS_ATT = 512
TS_MLP = 512
FF_CHUNK = 1024
CONV_ROWS = 32
HALO_A = 8
HALO_B = 32

VMEM_LIMIT = 56 * 1024 * 1024


def _rms(x, g):
    ms = jnp.mean(x * x, axis=-1, keepdims=True)
    return x * jax.lax.rsqrt(ms + EPS) * g


def _const_spec(shape):
    nd = len(shape)
    return pl.BlockSpec(shape, lambda *_: (0,) * nd, pipeline_mode=pl.Buffered(1))


def _kv_kernel(mem_ref, g_ref, wkv_ref, kt_ref, v_ref):
    m = _rms(mem_ref[0], g_ref[0]).astype(BF16)
    kv = jnp.dot(m, wkv_ref[0], preferred_element_type=F32)
    kt_ref[0, 0] = kv[:, :D_MODEL].T.astype(BF16)
    v_ref[0, 0] = kv[:, D_MODEL:].astype(BF16)


def _memory_kv(mem, norm_mem_g, w_kv):
    depth = w_kv.shape[0]
    batch = mem.shape[0]
    return pl.pallas_call(
        _kv_kernel,
        grid=(depth, batch),
        in_specs=[
            pl.BlockSpec((1, N_MEM, D_MODEL), lambda l, b: (b, 0, 0)),
            pl.BlockSpec((1, 1, D_MODEL), lambda l, b: (l, 0, 0)),
            pl.BlockSpec((1, D_MODEL, 2 * D_MODEL), lambda l, b: (l, 0, 0)),
        ],
        out_specs=[
            pl.BlockSpec((1, 1, D_MODEL, N_MEM), lambda l, b: (l, b, 0, 0)),
            pl.BlockSpec((1, 1, N_MEM, D_MODEL), lambda l, b: (l, b, 0, 0)),
        ],
        out_shape=[
            jax.ShapeDtypeStruct((depth, batch, D_MODEL, N_MEM), BF16),
            jax.ShapeDtypeStruct((depth, batch, N_MEM, D_MODEL), BF16),
        ],
        compiler_params=pltpu.CompilerParams(
            dimension_semantics=("arbitrary", "arbitrary"), vmem_limit_bytes=VMEM_LIMIT),
        name="memory_kv",
    )(mem, norm_mem_g.reshape(depth, 1, D_MODEL), w_kv)


def _mixer_kernel(h_ref, g_ref, win_ref, wa_ref, wb_ref, bias_ref, lng_ref, lnb_ref, wout_ref,
                  o_ref, u_ref, exta_ref, extb_ref, y_ref):
    ts = TS_MIX

    @pl.when(pl.program_id(1) == 0)
    def _():
        exta_ref[0:HALO_A, :] = jnp.zeros((HALO_A, D_A), F32)
        extb_ref[0:HALO_B, :] = jnp.zeros((HALO_B, D_B), F32)

    u_ref[...] = _rms(h_ref[0], g_ref[...]).astype(BF16)

    def proj(i):
        return jnp.dot(u_ref[...], win_ref[:, i * D_A:(i + 1) * D_A], preferred_element_type=F32)

    exta_ref[HALO_A:HALO_A + ts, :] = proj(1) * proj(2)
    b_a = proj(0)
    conv_a = wa_ref[0:1, :] * exta_ref[HALO_A - 2:HALO_A - 2 + ts, :]
    conv_a += wa_ref[1:2, :] * exta_ref[HALO_A - 1:HALO_A - 1 + ts, :]
    conv_a += wa_ref[2:3, :] * exta_ref[HALO_A:HALO_A + ts, :]
    y_ref[:, 0:D_A] = (b_a * conv_a).astype(BF16)

    extb_ref[HALO_B:HALO_B + ts, :] = proj(3) * jax.nn.sigmoid(proj(4))
    base = HALO_B - (K_B - 1)
    for r0 in range(0, ts, CONV_ROWS):
        acc = wb_ref[0:1, :] * extb_ref[r0 + base:r0 + base + CONV_ROWS, :]
        for k in range(1, K_B):
            acc += wb_ref[k:k + 1, :] * extb_ref[r0 + base + k:r0 + base + k + CONV_ROWS, :]
        cb = acc + bias_ref[...]
        mu = jnp.mean(cb, axis=-1, keepdims=True)
        d = cb - mu
        var = jnp.mean(d * d, axis=-1, keepdims=True)
        ln = d * jax.lax.rsqrt(var + EPS) * lng_ref[...] + lnb_ref[...]
        y_ref[r0:r0 + CONV_ROWS, D_A:] = (ln * jax.nn.sigmoid(ln)).astype(BF16)

    o_ref[0] = h_ref[0] + jnp.dot(y_ref[...], wout_ref[...], preferred_element_type=F32)

    exta_ref[0:HALO_A, :] = exta_ref[ts:ts + HALO_A, :]
    extb_ref[0:HALO_B, :] = extb_ref[ts:ts + HALO_B, :]


def _mixer(h, g, w_in, conv_a_w, conv_b_w, conv_b_bias, ln_g, ln_b, w_out):
    batch, seq, _ = h.shape
    ts = TS_MIX
    d_in = w_in.shape[1]
    return pl.pallas_call(
        _mixer_kernel,
        grid=(batch, seq // ts),
        in_specs=[
            pl.BlockSpec((1, ts, D_MODEL), lambda b, s: (b, s, 0)),
            _const_spec((1, D_MODEL)),
            _const_spec((D_MODEL, d_in)),
            _const_spec((K_A, D_A)),
            _const_spec((K_B, D_B)),
            _const_spec((1, D_B)),
            _const_spec((1, D_B)),
            _const_spec((1, D_B)),
            _const_spec((D_MODEL, D_MODEL)),
        ],
        out_specs=pl.BlockSpec((1, ts, D_MODEL), lambda b, s: (b, s, 0)),
        out_shape=jax.ShapeDtypeStruct(h.shape, F32),
        scratch_shapes=[
            pltpu.VMEM((ts, D_MODEL), BF16),
            pltpu.VMEM((HALO_A + ts, D_A), F32),
            pltpu.VMEM((HALO_B + ts, D_B), F32),
            pltpu.VMEM((ts, D_MODEL), BF16),
        ],
        compiler_params=pltpu.CompilerParams(
            dimension_semantics=("arbitrary", "arbitrary"), vmem_limit_bytes=VMEM_LIMIT),
        name="mixer",
    )(h, g.reshape(1, D_MODEL), w_in, conv_a_w, conv_b_w, conv_b_bias.reshape(1, D_B),
      ln_g.reshape(1, D_B), ln_b.reshape(1, D_B), w_out)


def _xattn_kernel(h_ref, g_ref, wq_ref, kt_ref, v_ref, wxo_ref, o_ref, q_ref, a_ref):
    scale = 1.0 / math.sqrt(XHEAD_DIM)
    u = _rms(h_ref[0], g_ref[...]).astype(BF16)
    q_ref[...] = jnp.dot(u, wq_ref[...], preferred_element_type=F32).astype(BF16)
    for hd in range(N_XHEADS):
        cols = slice(hd * XHEAD_DIM, (hd + 1) * XHEAD_DIM)
        s = jnp.dot(q_ref[:, cols], kt_ref[0, cols, :], preferred_element_type=F32) * scale
        e = jnp.exp(s - jnp.max(s, axis=-1, keepdims=True))
        p = e / jnp.sum(e, axis=-1, keepdims=True)
        a_ref[:, cols] = jnp.dot(p.astype(BF16), v_ref[0, :, cols],
                                 preferred_element_type=F32).astype(BF16)
    o_ref[0] = h_ref[0] + jnp.dot(a_ref[...], wxo_ref[...], preferred_element_type=F32)


def _xattn(h, g, w_q, kt, v, w_xo):
    batch, seq, _ = h.shape
    ts = TS_ATT
    return pl.pallas_call(
        _xattn_kernel,
        grid=(batch, seq // ts),
        in_specs=[
            pl.BlockSpec((1, ts, D_MODEL), lambda b, s: (b, s, 0)),
            _const_spec((1, D_MODEL)),
            _const_spec((D_MODEL, D_MODEL)),
            pl.BlockSpec((1, D_MODEL, N_MEM), lambda b, s: (b, 0, 0)),
            pl.BlockSpec((1, N_MEM, D_MODEL), lambda b, s: (b, 0, 0)),
            _const_spec((D_MODEL, D_MODEL)),
        ],
        out_specs=pl.BlockSpec((1, ts, D_MODEL), lambda b, s: (b, s, 0)),
        out_shape=jax.ShapeDtypeStruct(h.shape, F32),
        scratch_shapes=[
            pltpu.VMEM((ts, D_MODEL), BF16),
            pltpu.VMEM((ts, D_MODEL), BF16),
        ],
        compiler_params=pltpu.CompilerParams(
            dimension_semantics=("arbitrary", "arbitrary"), vmem_limit_bytes=VMEM_LIMIT),
        name="xattn",
    )(h, g.reshape(1, D_MODEL), w_q, kt, v, w_xo)


def _mlp_kernel(h_ref, g_ref, wup_ref, wdown_ref, gf_ref, o_ref, u_ref, *, final_norm):
    u_ref[...] = _rms(h_ref[...], g_ref[...]).astype(BF16)
    acc = h_ref[...]
    for c0 in range(0, D_FF, FF_CHUNK):
        t = jnp.dot(u_ref[...], wup_ref[:, c0:c0 + FF_CHUNK], preferred_element_type=F32)
        t = jnp.maximum(t, 0.0)
        acc += jnp.dot((t * t).astype(BF16), wdown_ref[c0:c0 + FF_CHUNK, :],
                       preferred_element_type=F32)
    if final_norm:
        acc = _rms(acc, gf_ref[...])
    o_ref[...] = acc


def _mlp(h2d, g, w_up, w_down, final_g, final_norm):
    rows = h2d.shape[0]
    tm = TS_MLP
    return pl.pallas_call(
        functools.partial(_mlp_kernel, final_norm=final_norm),
        grid=(rows // tm,),
        in_specs=[
            pl.BlockSpec((tm, D_MODEL), lambda i: (i, 0)),
            _const_spec((1, D_MODEL)),
            _const_spec((D_MODEL, D_FF)),
            _const_spec((D_FF, D_MODEL)),
            _const_spec((1, D_MODEL)),
        ],
        out_specs=pl.BlockSpec((tm, D_MODEL), lambda i: (i, 0)),
        out_shape=jax.ShapeDtypeStruct(h2d.shape, F32),
        scratch_shapes=[pltpu.VMEM((tm, D_MODEL), BF16)],
        compiler_params=pltpu.CompilerParams(
            dimension_semantics=("arbitrary",), vmem_limit_bytes=VMEM_LIMIT),
        name="mlp",
    )(h2d, g.reshape(1, D_MODEL), w_up, w_down, final_g.reshape(1, D_MODEL))


def kernel(x, mem, norm_mix_g, w_in, conv_a_w, conv_b_w, conv_b_bias, ln_b_g, ln_b_b, w_out,
           norm_x_g, norm_mem_g, w_q, w_kv, w_xo, norm_ffn_g, w_up, w_down, final_g):
    batch, seq, d = x.shape
    depth = w_in.shape[0]
    assert d == D_MODEL and seq % TS_MIX == 0 and seq % TS_ATT == 0
    assert (batch * seq) % TS_MLP == 0

    w_in, w_out, w_q, w_kv, w_xo, w_up, w_down = (
        w.astype(BF16) for w in (w_in, w_out, w_q, w_kv, w_xo, w_up, w_down))

    kt, v = _memory_kv(mem, norm_mem_g, w_kv)

    h = x
    for l in range(depth):
        h = _mixer(h, norm_mix_g[l], w_in[l], conv_a_w[l], conv_b_w[l], conv_b_bias[l],
                   ln_b_g[l], ln_b_b[l], w_out[l])
        h = _xattn(h, norm_x_g[l], w_q[l], kt[l], v[l], w_xo[l])
        h = _mlp(h.reshape(batch * seq, d), norm_ffn_g[l], w_up[l], w_down[l], final_g,
                 final_norm=(l == depth - 1)).reshape(batch, seq, d)
    return h
```
